```python
import math
import jax
import jax.numpy as jnp
from jax import lax
import numpy as np

D_MODEL = 1024
BATCH = 8
SEQ = 8192
DEPTH = 1

EPS = 1e-6
NEG_INF = -1e30
DN_HEADS = 4
DN_DK = 128
DN_DV = 128
DN_KEY_W = DN_HEADS * DN_DK
DN_VAL_W = DN_HEADS * DN_DV
DN_CONV = 4
DN_CHUNK = 64
N_DIR = 2
DA_GROUPS = ((128, 1), (512, 4), (2048, 16))
DA_HEADS = 4
DA_HD = 128
DA_W = len(DA_GROUPS) * DA_HEADS * DA_HD
RP_BUCKETS = 32
RP_MAX_DIST = 1024
PEER_HEADS = 8
PEER_NKEYS = 128
PEER_QDIM = 256
PEER_TOPK = 16
PEER_EXPERTS = PEER_NKEYS * PEER_NKEYS
PEER_TOKEN_BLOCK = 128

IN_WIDTHS = (DN_KEY_W, DN_KEY_W, DN_VAL_W, DN_VAL_W, N_DIR * DN_HEADS, N_DIR * DN_HEADS,
             DA_W, DA_W, DA_W, D_MODEL, D_MODEL)

kernel_name = 'hybrid_deltanet_dilated_peer_encoder'


def rmsnorm(x, g):
    xf = x.astype(jnp.float32)
    y = xf * lax.rsqrt(jnp.mean(xf * xf, axis=-1, keepdims=True) + EPS)
    return (y * g.astype(jnp.float32)).astype(x.dtype)


def l2norm(x):
    return x * lax.rsqrt(jnp.sum(x * x, axis=-1, keepdims=True) + EPS)


def split_columns(t, widths):
    out, start = [], 0
    for w in widths:
        out.append(t[..., start:start + w])
        start += w
    return out


def centred_depthwise_conv(x, w):
    k, c = w.shape
    left = k // 2
    return lax.conv_general_dilated(x, w[:, None, :], window_strides=(1,), padding=[(left, k - 1 - left)],
                                    dimension_numbers=('NWC', 'WIO', 'NWC'), feature_group_count=c)


def gated_delta_rule(q, k, v, g, beta):
    r, s, dk = q.shape
    dv = v.shape[-1]
    c = DN_CHUNK
    n = s // c
    q = (q * (dk ** -0.5)).reshape(r, n, c, dk)
    k = k.reshape(r, n, c, dk)
    v = v.reshape(r, n, c, dv)
    beta = beta.reshape(r, n, c)
    gcum = jnp.cumsum(g.reshape(r, n, c), axis=-1)
    diff = gcum[..., :, None] - gcum[..., None, :]
    lower = jnp.tril(jnp.ones((c, c), dtype=bool))
    strict = jnp.tril(jnp.ones((c, c), dtype=bool), -1)
    decay = jnp.where(lower, jnp.exp(jnp.where(lower, diff, 0.0)), 0.0)
    kb = k * beta[..., None]
    lmat = jnp.where(strict, jnp.einsum('rnid,rnjd->rnij', kb, k) * decay, 0.0)
    amat = jnp.eye(c, dtype=jnp.float32) + lmat
    rhs = jnp.concatenate([v * beta[..., None], kb * jnp.exp(gcum)[..., None]], axis=-1)
    sol = lax.linalg.triangular_solve(amat, rhs, left_side=True, lower=True, unit_diagonal=True)
    u, w = sol[..., :dv], sol[..., dv:]
    intra = jnp.einsum('rnid,rnjd->rnij', q, k) * decay
    q_dec = q * jnp.exp(gcum)[..., None]
    k_dec = k * jnp.exp(gcum[..., -1:] - gcum)[..., None]
    g_last = jnp.exp(gcum[..., -1])

    def step(state, inp):
        u_i, w_i, intra_i, qd_i, kd_i, gl_i = inp
        v_new = u_i - jnp.einsum('rcd,rde->rce', w_i, state)
        out = jnp.einsum('rcd,rde->rce', qd_i, state) + jnp.einsum('rij,rje->rie', intra_i, v_new)
        state = state * gl_i[:, None, None] + jnp.einsum('rcd,rce->rde', kd_i, v_new)
        return state, out

    xs = tuple(jnp.moveaxis(a, 1, 0) for a in (u, w, intra, q_dec, k_dec, g_last))
    _, out = lax.scan(step, jnp.zeros((r, dk, dv), jnp.float32), xs)
    return jnp.moveaxis(out, 0, 1).reshape(r, s, dv)


def deltanet_branch(q, k, v, z, a, b, conv_w, a_log, dt_bias, out_norm):
    bsz, s, _ = q.shape
    f32 = jnp.float32
    qkv = jax.nn.silu(centred_depthwise_conv(jnp.concatenate([q, k, v], axis=-1), conv_w))
    q, k, v = split_columns(qkv, (DN_KEY_W, DN_KEY_W, DN_VAL_W))

    def heads(t, d):
        return t.reshape(bsz, s, DN_HEADS, d).transpose(0, 2, 1, 3).astype(f32)

    def per_dir(t):
        return t.reshape(bsz, s, N_DIR, DN_HEADS).transpose(2, 0, 3, 1).astype(f32)

    q, k, v = l2norm(heads(q, DN_DK)), l2norm(heads(k, DN_DK)), heads(v, DN_DV)
    g = -jnp.exp(a_log.astype(f32))[:, None, :, None] * jax.nn.softplus(per_dir(a) + dt_bias.astype(f32)[:, None, :, None])
    beta = jax.nn.sigmoid(per_dir(b))
    qs = jnp.stack([q, jnp.flip(q, 2)])
    ks = jnp.stack([k, jnp.flip(k, 2)])
    vs = jnp.stack([v, jnp.flip(v, 2)])
    gs = jnp.stack([g[0], jnp.flip(g[1], -1)])
    bs = jnp.stack([beta[0], jnp.flip(beta[1], -1)])
    rows = N_DIR * bsz * DN_HEADS
    o = gated_delta_rule(qs.reshape(rows, s, DN_DK), ks.reshape(rows, s, DN_DK), vs.reshape(rows, s, DN_DV),
                         gs.reshape(rows, s), bs.reshape(rows, s)).reshape(N_DIR, bsz, DN_HEADS, s, DN_DV)
    o = (o[0] + jnp.flip(o[1], 2)).transpose(0, 2, 1, 3)
    o = rmsnorm(o, out_norm) * jax.nn.silu(z.reshape(bsz, s, DN_HEADS, DN_DV).astype(f32))
    return o.reshape(bsz, s, DN_VAL_W)


def t5_bucket(rel):
    half = RP_BUCKETS // 2
    max_exact = half // 2
    n = np.abs(rel)
    large = max_exact + (np.log(np.maximum(n, 1) / max_exact) / math.log(RP_MAX_DIST / max_exact)
                         * (half - max_exact)).astype(np.int64)
    large = np.minimum(large, half - 1)
    return np.where(rel > 0, half, 0) + np.where(n < max_exact, n, large)


def dilated_group_attention(q, k, v, bias_table, window, dil):
    bsz, h, s, hd = q.shape
    side = window // (2 * dil)
    length = s // dil
    nb = -(-length // side)
    padded = nb * side

    def residue_blocks(t):
        t = t.reshape(bsz, h, length, dil, hd).transpose(0, 1, 3, 2, 4)
        t = jnp.pad(t, ((0, 0), (0, 0), (0, 0), (0, padded - length), (0, 0)))
        return t.reshape(bsz, h, dil, nb, side, hd)

    def with_neighbours(t):
        tp = jnp.pad(t, ((0, 0), (0, 0), (0, 0), (1, 1), (0, 0), (0, 0)))
        return jnp.concatenate([tp[:, :, :, :-2], tp[:, :, :, 1:-1], tp[:, :, :, 2:]], axis=4)

    qb = residue_blocks(q)
    kn = with_neighbours(residue_blocks(k))
    vn = with_neighbours(residue_blocks(v)).astype(jnp.float32)
    rel = np.arange(3 * side)[None, :] - side - np.arange(side)[:, None]
    key_idx = np.arange(nb)[:, None, None] * side + np.arange(3 * side)[None, None, :] - side
    valid = (np.abs(rel) <= side)[None] & (key_idx >= 0) & (key_idx < length)
    bias = bias_table.astype(jnp.float32)[:, t5_bucket(rel * dil)]
    scores = jnp.einsum('bhrnqd,bhrnkd->bhrnqk', qb, kn, preferred_element_type=jnp.float32) * (hd ** -0.5)
    scores = jnp.where(valid, scores + bias[None, :, None, None], NEG_INF)
    m = jnp.max(scores, axis=-1, keepdims=True)
    p = jnp.exp(scores - m)
    l = jnp.sum(p, axis=-1)
    o = jnp.einsum('bhrnqk,bhrnkd->bhrnqd', p, vn) / l[..., None]
    lse = m[..., 0] + jnp.log(l)
    o = o.reshape(bsz, h, dil, padded, hd)[:, :, :, :length].transpose(0, 1, 3, 2, 4).reshape(bsz, h, s, hd)
    lse = lse.reshape(bsz, h, dil, padded)[..., :length].transpose(0, 1, 3, 2).reshape(bsz, h, s)
    return o, lse


def dilated_attention_branch(q, k, v, rel_bias):
    bsz, s, _ = q.shape

    def groups(t):
        return t.reshape(bsz, s, len(DA_GROUPS), DA_HEADS, DA_HD).transpose(2, 0, 3, 1, 4)

    qg, kg, vg = groups(q), groups(k), groups(v)
    outs, lses = [], []
    for gi, (window, dil) in enumerate(DA_GROUPS):
        table = rel_bias[:, gi * DA_HEADS:(gi + 1) * DA_HEADS].T
        o, lse = dilated_group_attention(qg[gi], kg[gi], vg[gi], table, window, dil)
        outs.append(o)
        lses.append(lse)
    wts = jax.nn.softmax(jnp.stack(lses), axis=0)
    o = jnp.einsum('gbhs,gbhsd->bhsd', wts, jnp.stack(outs))
    return o.transpose(0, 2, 1, 3).reshape(bsz, s, DA_HEADS * DA_HD)


def peer_ffn(xn, w_q, sub_keys, expert_u, expert_v):
    bsz, s, d = xn.shape
    t = bsz * s
    x2 = xn.reshape(t, d)
    qh = (x2 @ w_q).reshape(t, PEER_HEADS, 2, PEER_QDIM // 2)
    scores = jnp.einsum('thcd,hckd->thck', qh, sub_keys, preferred_element_type=jnp.float32)
    sv, si = lax.top_k(scores, PEER_TOPK)
    cand = sv[..., 0, :, None] + sv[..., 1, None, :]
    cv, ci = lax.top_k(cand.reshape(t, PEER_HEADS, PEER_TOPK * PEER_TOPK), PEER_TOPK)
    i1 = jnp.take_along_axis(si[..., 0, :], ci // PEER_TOPK, axis=-1)
    i2 = jnp.take_along_axis(si[..., 1, :], ci % PEER_TOPK, axis=-1)
    experts = (i1 * PEER_NKEYS + i2).reshape(t, PEER_HEADS * PEER_TOPK)
    gates = jax.nn.softmax(cv, axis=-1).reshape(t, PEER_HEADS * PEER_TOPK)

    def token_block(args):
        xb, eb, gb = args
        act = jax.nn.gelu(jnp.einsum('tkd,td->tk', expert_u[eb], xb, preferred_element_type=jnp.float32))
        return jnp.einsum('tk,tkd->td', act * gb, expert_v[eb].astype(jnp.float32)).astype(xb.dtype)

    nblk = t // PEER_TOKEN_BLOCK
    out = lax.map(token_block, (x2.reshape(nblk, PEER_TOKEN_BLOCK, d),
                                experts.reshape(nblk, PEER_TOKEN_BLOCK, -1),
                                gates.reshape(nblk, PEER_TOKEN_BLOCK, -1)))
    return out.reshape(bsz, s, d)


def setup_inputs(seed: int = 0) -> dict:
    key = jax.random.key(seed)
    ks = jax.random.split(key, 18)
    f32 = jnp.float32

    def normal(k, shape, scale):
        return jax.random.normal(k, shape, f32) * scale

    dt = jnp.exp(jax.random.uniform(ks[5], (DEPTH, N_DIR, DN_HEADS), f32, math.log(1e-3), math.log(1e-1)))
    return {
        'x': normal(ks[0], (BATCH, SEQ, D_MODEL), 1.0),
        'norm_mix': 1.0 + normal(ks[1], (DEPTH, D_MODEL), 0.02),
        'w_in': normal(ks[2], (DEPTH, D_MODEL, sum(IN_WIDTHS)), D_MODEL ** -0.5),
        'dn_conv': normal(ks[3], (DEPTH, DN_CONV, 2 * DN_KEY_W + DN_VAL_W), DN_CONV ** -0.5),
        'dn_a_log': jnp.log(jax.random.uniform(ks[4], (DEPTH, N_DIR, DN_HEADS), f32, 1.0, 16.0)),
        'dn_dt_bias': dt + jnp.log(-jnp.expm1(-dt)),
        'dn_out_norm': 1.0 + normal(ks[6], (DEPTH, DN_DV), 0.02),
        'rel_bias': normal(ks[7], (RP_BUCKETS, len(DA_GROUPS) * DA_HEADS), 0.2),
        'w_branch_dn': normal(ks[8], (DEPTH, DN_VAL_W, D_MODEL), DN_VAL_W ** -0.5),
        'w_branch_da': normal(ks[9], (DEPTH, DA_HEADS * DA_HD, D_MODEL), (DA_HEADS * DA_HD) ** -0.5),
        'w_out': normal(ks[10], (DEPTH, D_MODEL, D_MODEL), D_MODEL ** -0.5),
        'norm_ffn': 1.0 + normal(ks[11], (DEPTH, D_MODEL), 0.02),
        'peer_wq': normal(ks[12], (DEPTH, D_MODEL, PEER_HEADS * PEER_QDIM), D_MODEL ** -0.5),
        'peer_sub_keys': normal(ks[13], (DEPTH, PEER_HEADS, 2, PEER_NKEYS, PEER_QDIM // 2), (PEER_QDIM // 2) ** -0.5),
        'peer_u': normal(ks[14], (DEPTH, PEER_EXPERTS, D_MODEL), D_MODEL ** -0.5),
        'peer_v': normal(ks[15], (DEPTH, PEER_EXPERTS, D_MODEL), PEER_HEADS ** -0.5),
        'norm_final': 1.0 + normal(ks[16], (D_MODEL,), 0.02),
    }


def reference(x, norm_mix, w_in, dn_conv, dn_a_log, dn_dt_bias, dn_out_norm, rel_bias, w_branch_dn,
              w_branch_da, w_out, norm_ffn, peer_wq, peer_sub_keys, peer_u, peer_v, norm_final):
    h = x
    for layer in range(DEPTH):
        xn = rmsnorm(h, norm_mix[layer])
        (dq, dk, dv, dz, da, db, aq, ak, av, gate_dn, gate_da) = split_columns(xn @ w_in[layer], IN_WIDTHS)
        o_dn = deltanet_branch(dq, dk, dv, dz, da, db, dn_conv[layer], dn_a_log[layer], dn_dt_bias[layer],
                               dn_out_norm[layer]).astype(h.dtype)
        o_da = dilated_attention_branch(aq, ak, av, rel_bias).astype(h.dtype)
        merged = (jax.nn.sigmoid(gate_dn) * (o_dn @ w_branch_dn[layer])
                  + jax.nn.sigmoid(gate_da) * (o_da @ w_branch_da[layer]))
        h = h + merged @ w_out[layer]
        h = h + peer_ffn(rmsnorm(h, norm_ffn[layer]), peer_wq[layer], peer_sub_keys[layer],
                         peer_u[layer], peer_v[layer])
    return rmsnorm(h, norm_final)
```

```python
import functools
import math

import jax
import jax.numpy as jnp
import numpy as np
from jax import lax
from jax.experimental import pallas as pl
from jax.experimental.pallas import tpu as pltpu

f32 = jnp.float32
bf16 = jnp.bfloat16

EPS = 1e-6
NEG_INF = -1e30
DN_HEADS = 4
DN_D = 128
DN_W = DN_HEADS * DN_D
DN_CONV = 4
DN_CHUNK = 64
DA_GROUPS = ((128, 1), (512, 4), (2048, 16))
DA_HEADS = 4
DA_HD = 128
DA_GW = DA_HEADS * DA_HD
RP_BUCKETS = 32
RP_MAX_DIST = 1024
PEER_HEADS = 8
PEER_NKEYS = 128
PEER_HALF = 128
PEER_TOPK = 16
LANES = 128
VMEM_LIMIT = 56 * 1024 * 1024


def _dot(a, b):
    return jnp.dot(a, b, preferred_element_type=f32)


def _dot_nt(a, b):
    return lax.dot_general(a, b, (((1,), (1,)), ((), ())), preferred_element_type=f32)


def _bmm(a, b):
    return jnp.einsum('cij,cjk->cik', a, b, preferred_element_type=f32)


def _bmm_nt(a, b):
    return jnp.einsum('cik,cjk->cij', a, b, preferred_element_type=f32)


def _sigmoid(x):
    return 1.0 / (1.0 + jnp.exp(-x))


def _params(*sem):
    return pltpu.CompilerParams(dimension_semantics=sem, vmem_limit_bytes=VMEM_LIMIT)


def _resident(shape):
    nd = len(shape)
    return pl.BlockSpec(shape, lambda *_: (0,) * nd, pipeline_mode=pl.Buffered(1))


def _inproj_body(x_ref, g_ref, w_ref, dn_ref, ab_ref, a0_ref, a1_ref, a2_ref, gt_ref, xn_scr, st_scr):
    tm = x_ref.shape[0]
    x = x_ref[...]
    ms = jnp.mean(x * x, axis=-1, keepdims=True)
    xn_scr[...] = (x * lax.rsqrt(ms + EPS) * g_ref[...]).astype(bf16)
    xb = xn_scr[...]
    for c in range(4):
        dn_ref[:, c * DN_W:(c + 1) * DN_W] = _dot(xb, w_ref[:, c * DN_W:(c + 1) * DN_W]).astype(bf16)
    base = 4 * DN_W
    ab_ref[...] = _dot(xb, w_ref[:, base:base + LANES])
    base += LANES
    for ref, (_, dil) in zip((a0_ref, a1_ref, a2_ref), DA_GROUPS):
        for c in range(3):
            acc = _dot(xb, w_ref[:, base + c * DA_GW:base + (c + 1) * DA_GW])
            if dil == 1:
                ref[0, 0, :, c * DA_GW:(c + 1) * DA_GW] = acc.astype(bf16)
            else:
                for hh in range(DA_HEADS):
                    st_scr[hh] = acc[:, hh * DA_HD:(hh + 1) * DA_HD]
                for r in range(dil):
                    for hh in range(DA_HEADS):
                        lo = c * DA_GW + hh * DA_HD
                        ref[0, r, :, lo:lo + DA_HD] = st_scr[hh, pl.ds(r, tm // dil, stride=dil), :].astype(bf16)
        base += 3 * DA_GW
    for c in range(4):
        gt_ref[:, c * DN_W:(c + 1) * DN_W] = _sigmoid(_dot(xb, w_ref[:, base + c * DN_W:base + (c + 1) * DN_W])).astype(bf16)


def _inproj(x2, g, w, bsz, seq, tm):
    t, d = x2.shape
    nw = w.shape[1]
    spb = seq // tm
    outs = [jax.ShapeDtypeStruct((t, 4 * DN_W), bf16), jax.ShapeDtypeStruct((t, LANES), f32)]
    specs = [pl.BlockSpec((tm, 4 * DN_W), lambda i: (i, 0)), pl.BlockSpec((tm, LANES), lambda i: (i, 0))]
    for _, dil in DA_GROUPS:
        outs.append(jax.ShapeDtypeStruct((bsz, dil, seq // dil, 3 * DA_GW), bf16))
        specs.append(pl.BlockSpec((1, dil, tm // dil, 3 * DA_GW), lambda i: (i // spb, 0, i % spb, 0)))
    outs.append(jax.ShapeDtypeStruct((t, 4 * DN_W), bf16))
    specs.append(pl.BlockSpec((tm, 4 * DN_W), lambda i: (i, 0)))
    return pl.pallas_call(
        _inproj_body, out_shape=outs, grid=(t // tm,),
        in_specs=[pl.BlockSpec((tm, d), lambda i: (i, 0)), _resident((1, d)), _resident((d, nw))],
        out_specs=specs,
        scratch_shapes=[pltpu.VMEM((tm, d), bf16), pltpu.VMEM((DA_HEADS, tm, DA_HD), f32)],
        compiler_params=_params("parallel"), name="inproj")(x2, g, w)


def _dnprep_body(main_ref, prev_ref, next_ref, ab_ref, cw_ref, al_ref, dtb_ref, qkv_ref, gb_ref, buf):
    i = pl.program_id(1)
    last = pl.num_programs(1) - 1
    sb = main_ref.shape[1]
    halo = prev_ref.shape[1]
    prev = prev_ref[0].astype(f32)[halo - 8:]
    nxt = next_ref[0].astype(f32)[:8]
    buf[0:8] = jnp.where(i > 0, prev, 0.0)
    buf[8:8 + sb] = main_ref[0].astype(f32)
    buf[8 + sb:16 + sb] = jnp.where(i < last, nxt, 0.0)
    cw = cw_ref[...]
    for hh in range(3 * DN_HEADS):
        cols = slice(hh * DN_D, (hh + 1) * DN_D)
        y = (cw[0:1, cols] * buf[pl.ds(6, sb), cols] + cw[1:2, cols] * buf[pl.ds(7, sb), cols]
             + cw[2:3, cols] * buf[pl.ds(8, sb), cols] + cw[3:4, cols] * buf[pl.ds(9, sb), cols])
        y = y * _sigmoid(y)
        if hh < 2 * DN_HEADS:
            scale = lax.rsqrt(jnp.sum(y * y, axis=-1, keepdims=True) + EPS)
            if hh < DN_HEADS:
                scale = scale * (DN_D ** -0.5)
            y = y * scale
        qkv_ref[0, :, cols] = y.astype(bf16)
    ab = ab_ref[0]
    z = ab + dtb_ref[...]
    softplus = jnp.maximum(z, 0.0) + jnp.log(1.0 + jnp.exp(-jnp.abs(z)))
    g = -jnp.exp(al_ref[...]) * softplus
    lane = lax.broadcasted_iota(jnp.int32, ab.shape, 1)
    gb_ref[0] = jnp.where(lane < 2 * DN_HEADS, g, _sigmoid(ab))


def _dnprep(dn3, ab3, cw, al, dtb, sb):
    bsz, seq, _ = dn3.shape
    nblk = seq // sb
    halo = 16
    per = sb // halo
    return pl.pallas_call(
        _dnprep_body,
        out_shape=[jax.ShapeDtypeStruct((bsz, seq, 3 * DN_W), bf16), jax.ShapeDtypeStruct((bsz, seq, LANES), f32)],
        grid=(bsz, nblk),
        in_specs=[pl.BlockSpec((1, sb, 3 * DN_W), lambda b, i: (b, i, 0)),
                  pl.BlockSpec((1, halo, 3 * DN_W), lambda b, i: (b, jnp.maximum(i * per - 1, 0), 0)),
                  pl.BlockSpec((1, halo, 3 * DN_W), lambda b, i: (b, jnp.minimum((i + 1) * per, seq // halo - 1), 0)),
                  pl.BlockSpec((1, sb, LANES), lambda b, i: (b, i, 0)),
                  _resident((DN_CONV, 3 * DN_W)), _resident((1, LANES)), _resident((1, LANES))],
        out_specs=[pl.BlockSpec((1, sb, 3 * DN_W), lambda b, i: (b, i, 0)),
                   pl.BlockSpec((1, sb, LANES), lambda b, i: (b, i, 0))],
        scratch_shapes=[pltpu.VMEM((sb + 16, 3 * DN_W), f32)],
        compiler_params=_params("parallel", "parallel"), name="dnprep")(dn3, dn3, dn3, ab3, cw, al, dtb)


def _unit_tri_inverse(lmat, rb, cb):
    n = lmat.shape[-1]
    eye = (rb == cb).astype(f32)
    nb = jnp.where(rb // 8 == cb // 8, lmat, 0.0).astype(bf16)
    n2 = _bmm(nb, nb)
    x = eye - nb.astype(f32)
    x = x + _bmm(x.astype(bf16), n2.astype(bf16))
    n4 = _bmm(n2.astype(bf16), n2.astype(bf16))
    x = x + _bmm(x.astype(bf16), n4.astype(bf16))
    b = 8
    while b < n:
        coup = jnp.where((rb // (2 * b) == cb // (2 * b)) & (rb // b != cb // b), lmat, 0.0).astype(bf16)
        xb = x.astype(bf16)
        x = x - _bmm(_bmm(xb, coup).astype(bf16), xb)
        b *= 2
    return x


def _delta_body(qkv_ref, gb_ref, o_ref, s_scr, u_scr, wq_scr, ik_scr, gl_scr, *, dirn):
    blk = pl.program_id(1)
    sb = qkv_ref.shape[1]
    ck = DN_CHUNK
    nc = sb // ck

    @pl.when(blk == 0)
    def _():
        s_scr[...] = jnp.zeros_like(s_scr)

    rb = lax.broadcasted_iota(jnp.int32, (ck, ck), 0)
    cb = lax.broadcasted_iota(jnp.int32, (ck, ck), 1)
    incl = (rb >= cb) if dirn == 0 else (rb <= cb)
    strict = (rb > cb) if dirn == 0 else (rb < cb)
    tri = jnp.broadcast_to(incl.astype(f32), (nc, ck, ck))
    gb3 = gb_ref[0].reshape(nc, ck, LANES)
    gcum = _bmm(tri, gb3)
    gcum_t = jnp.swapaxes(gcum, 1, 2)
    gtot = jnp.sum(gb3, axis=1, keepdims=True)

    for h in range(DN_HEADS):
        lg = dirn * DN_HEADS + h
        lbeta = 2 * DN_HEADS + lg
        gc = gcum[:, :, lg:lg + 1]
        gcr = gcum_t[:, lg:lg + 1, :]
        gt = gtot[:, :, lg:lg + 1]
        beta = gb3[:, :, lbeta:lbeta + 1]
        q = qkv_ref[0, :, h * DN_D:(h + 1) * DN_D].reshape(nc, ck, DN_D)
        k = qkv_ref[0, :, DN_W + h * DN_D:DN_W + (h + 1) * DN_D].reshape(nc, ck, DN_D)
        v = qkv_ref[0, :, 2 * DN_W + h * DN_D:2 * DN_W + (h + 1) * DN_D].reshape(nc, ck, DN_D)
        kf = k.astype(f32)
        kb = kf * beta
        egc = jnp.exp(gc)
        decay = jnp.where(incl, jnp.exp(jnp.where(incl, gc - gcr, 0.0)), 0.0)
        lmat = jnp.where(strict, _bmm_nt(kb.astype(bf16), k) * decay, 0.0)
        tinv = _unit_tri_inverse(lmat, rb, cb)
        rhs = jnp.concatenate([v.astype(f32) * beta, kb * egc], axis=-1).astype(bf16)
        sol = _bmm(tinv.astype(bf16), rhs)
        intra = _bmm_nt(q, k) * decay
        q_dec = q.astype(f32) * egc
        k_dec_t = jnp.swapaxes(kf * jnp.exp(gt - gc), 1, 2)
        u_scr[h] = sol[:, :, :DN_D]
        wq_scr[h, :, :ck, :] = sol[:, :, DN_D:].astype(bf16)
        wq_scr[h, :, ck:, :] = q_dec.astype(bf16)
        ik_scr[h, :, :ck, :] = intra.astype(bf16)
        ik_scr[h, :, ck:, :] = k_dec_t.astype(bf16)
        gl_scr[h] = jnp.broadcast_to(jnp.exp(gt), (nc, DN_D, LANES))

    order = range(nc) if dirn == 0 else range(nc - 1, -1, -1)
    for c in order:
        for h in range(DN_HEADS):
            s = s_scr[h]
            a1 = _dot(wq_scr[h, c], s.astype(bf16))
            v_new = u_scr[h, c] - a1[:ck]
            a2 = _dot(ik_scr[h, c], v_new.astype(bf16))
            o_ref[0, c * ck:(c + 1) * ck, h * DN_D:(h + 1) * DN_D] = a1[ck:] + a2[:ck]
            s_scr[h] = s * gl_scr[h, c] + a2[ck:]


def _delta_scan(qkv3, gb3, dirn, sb):
    bsz, seq, _ = qkv3.shape
    nblk = seq // sb
    nc = sb // DN_CHUNK
    if dirn == 0:
        imap = lambda b, i: (b, i, 0)
    else:
        imap = lambda b, i: (b, nblk - 1 - i, 0)
    return pl.pallas_call(
        functools.partial(_delta_body, dirn=dirn),
        out_shape=jax.ShapeDtypeStruct((bsz, seq, DN_W), f32),
        grid=(bsz, nblk),
        in_specs=[pl.BlockSpec((1, sb, 3 * DN_W), imap), pl.BlockSpec((1, sb, LANES), imap)],
        out_specs=pl.BlockSpec((1, sb, DN_W), imap),
        scratch_shapes=[pltpu.VMEM((DN_HEADS, DN_D, DN_D), f32),
                        pltpu.VMEM((DN_HEADS, nc, DN_CHUNK, DN_D), f32),
                        pltpu.VMEM((DN_HEADS, nc, 2 * DN_CHUNK, DN_D), bf16),
                        pltpu.VMEM((DN_HEADS, nc, DN_CHUNK + DN_D, DN_CHUNK), bf16),
                        pltpu.VMEM((DN_HEADS, nc, DN_D, LANES), f32)],
        compiler_params=_params("parallel", "arbitrary"), name=f"delta_scan_d{dirn}")(qkv3, gb3)


def _t5_bucket(rel):
    half = RP_BUCKETS // 2
    max_exact = half // 2
    n = np.abs(rel)
    large = max_exact + (np.log(np.maximum(n, 1) / max_exact) / math.log(RP_MAX_DIST / max_exact)
                         * (half - max_exact)).astype(np.int64)
    large = np.minimum(large, half - 1)
    return np.where(rel > 0, half, 0) + np.where(n < max_exact, n, large)


def _attn_body(q_ref, kp_ref, km_ref, kn_ref, vp_ref, vm_ref, vn_ref, bias_ref, o_ref, lse_ref, *, side):
    i = pl.program_id(2)
    last = pl.num_programs(2) - 1
    tq = q_ref.shape[2]
    tk = tq + 2 * side
    col = lax.broadcasted_iota(jnp.int32, (tq, tk), 1)
    lo = jnp.where(i == 0, side, 0)
    hi = jnp.where(i == last, tq + side, tk)
    in_seq = (col >= lo) & (col < hi)
    lse_ref[0, 0] = jnp.zeros((tq, LANES), f32)
    for h in range(DA_HEADS):
        sl = slice(h * DA_HD, (h + 1) * DA_HD)
        q = q_ref[0, 0, :, sl]
        kcat = jnp.concatenate([kp_ref[0, 0, :, sl], km_ref[0, 0, :, sl], kn_ref[0, 0, :, sl]], axis=0)
        vcat = jnp.concatenate([vp_ref[0, 0, :, sl], vm_ref[0, 0, :, sl], vn_ref[0, 0, :, sl]], axis=0)
        s = _dot_nt(q, kcat) * (DA_HD ** -0.5) + bias_ref[h]
        s = jnp.where(in_seq, s, NEG_INF)
        m = jnp.max(s, axis=-1, keepdims=True)
        p = jnp.exp(s - m)
        l = jnp.sum(p, axis=-1, keepdims=True)
        o_ref[0, 0, :, sl] = (_dot(p.astype(bf16), vcat) / l).astype(bf16)
        lse_ref[0, 0, :, h:h + 1] = m + jnp.log(l)


def _dilated_attention(qkv, bias, window, dil, tq):
    bsz, _, length, _ = qkv.shape
    side = window // (2 * dil)
    per = tq // side
    nq = length // tq
    nside = length // side
    main = lambda c: pl.BlockSpec((1, 1, tq, DA_GW), lambda b, r, i: (b, r, i, c))
    prev = lambda c: pl.BlockSpec((1, 1, side, DA_GW), lambda b, r, i: (b, r, jnp.maximum(i * per - 1, 0), c))
    nxt = lambda c: pl.BlockSpec((1, 1, side, DA_GW), lambda b, r, i: (b, r, jnp.minimum((i + 1) * per, nside - 1), c))
    return pl.pallas_call(
        functools.partial(_attn_body, side=side),
        out_shape=[jax.ShapeDtypeStruct((bsz, dil, length, DA_GW), bf16),
                   jax.ShapeDtypeStruct((bsz, dil, length, LANES), f32)],
        grid=(bsz, dil, nq),
        in_specs=[main(0), prev(1), main(1), nxt(1), prev(2), main(2), nxt(2), _resident(bias.shape)],
        out_specs=[pl.BlockSpec((1, 1, tq, DA_GW), lambda b, r, i: (b, r, i, 0)),
                   pl.BlockSpec((1, 1, tq, LANES), lambda b, r, i: (b, r, i, 0))],
        compiler_params=_params("parallel", "parallel", "parallel"), name=f"dil_attn_d{dil}",
    )(qkv, qkv, qkv, qkv, qkv, qkv, qkv, bias)


def _attn_bias(rel_bias, gi, window, dil, tq):
    side = window // (2 * dil)
    rel = np.arange(tq + 2 * side)[None, :] - side - np.arange(tq)[:, None]
    table = rel_bias[:, gi * DA_HEADS:(gi + 1) * DA_HEADS].T.astype(f32)
    bias = table[:, _t5_bucket(rel * dil)]
    return jnp.where(jnp.asarray(np.abs(rel) <= side)[None], bias, NEG_INF)


def _merge_body(x_ref, of_ref, ob_ref, z_ref, gt_ref, o0_ref, l0_ref, o1_ref, l1_ref, o2_ref, l2_ref,
                onorm_ref, wdn_ref, wda_ref, wout_ref, nffn_ref, wq_ref,
                h_ref, xn_ref, q_ref, o_scr, l_scr):
    tm = x_ref.shape[0]
    parts = []
    for h in range(DN_HEADS):
        sl = slice(h * DN_D, (h + 1) * DN_D)
        o = of_ref[:, sl] + ob_ref[:, sl]
        o = o * lax.rsqrt(jnp.mean(o * o, axis=-1, keepdims=True) + EPS) * onorm_ref[...]
        z = z_ref[:, sl].astype(f32)
        parts.append((o * (z * _sigmoid(z))).astype(bf16))
    o_dn = jnp.concatenate(parts, axis=-1)
    for gi, (o_ref, l_ref, (_, dil)) in enumerate(zip((o0_ref, o1_ref, o2_ref), (l0_ref, l1_ref, l2_ref), DA_GROUPS)):
        for r in range(dil):
            rows = pl.ds(r, tm // dil, stride=dil) if dil > 1 else slice(None)
            for h in range(DA_HEADS):
                o_scr[gi, h, rows, :] = o_ref[0, r, :, h * DA_HD:(h + 1) * DA_HD].astype(f32)
            l_scr[gi, rows, :] = l_ref[0, r]
    lses = [l_scr[gi] for gi in range(len(DA_GROUPS))]
    mx = jnp.maximum(jnp.maximum(lses[0], lses[1]), lses[2])
    ws = [jnp.exp(l - mx) for l in lses]
    inv = 1.0 / (ws[0] + ws[1] + ws[2])
    parts = []
    for h in range(DA_HEADS):
        acc = sum(ws[gi][:, h:h + 1] * o_scr[gi, h] for gi in range(len(DA_GROUPS)))
        parts.append((acc * inv[:, h:h + 1]).astype(bf16))
    o_da = jnp.concatenate(parts, axis=-1)
    d = x_ref.shape[1]
    merged = (gt_ref[:, :d].astype(f32) * _dot(o_dn, wdn_ref[...])
              + gt_ref[:, d:].astype(f32) * _dot(o_da, wda_ref[...]))
    hid = x_ref[...] + _dot(merged.astype(bf16), wout_ref[...])
    h_ref[...] = hid
    xn = (hid * lax.rsqrt(jnp.mean(hid * hid, axis=-1, keepdims=True) + EPS) * nffn_ref[...]).astype(bf16)
    xn_ref[...] = xn
    q_ref[...] = _dot(xn, wq_ref[...]).astype(bf16)


def _merge(x2, o_f, o_b, dn, gates, att, onorm, wdn, wda, wout, nffn, wq, bsz, seq, tm):
    t, d = x2.shape
    spb = seq // tm
    row = lambda w: pl.BlockSpec((tm, w), lambda i: (i, 0))
    in_specs = [row(d), row(DN_W), row(DN_W), pl.BlockSpec((tm, DN_W), lambda i: (i, 3)), row(2 * d)]
    args = [x2, o_f, o_b, dn, gates]
    for (o_g, lse_g), (_, dil) in zip(att, DA_GROUPS):
        in_specs.append(pl.BlockSpec((1, dil, tm // dil, DA_GW), lambda i: (i // spb, 0, i % spb, 0)))
        in_specs.append(pl.BlockSpec((1, dil, tm // dil, LANES), lambda i: (i // spb, 0, i % spb, 0)))
        args += [o_g, lse_g]
    weights = [onorm, wdn, wda, wout, nffn, wq]
    in_specs += [_resident(w.shape) for w in weights]
    nq = wq.shape[1]
    return pl.pallas_call(
        _merge_body,
        out_shape=[jax.ShapeDtypeStruct((t, d), f32), jax.ShapeDtypeStruct((t, d), bf16),
                   jax.ShapeDtypeStruct((t, nq), bf16)],
        grid=(t // tm,), in_specs=in_specs, out_specs=[row(d), row(d), row(nq)],
        scratch_shapes=[pltpu.VMEM((len(DA_GROUPS), DA_HEADS, tm, DA_HD), f32),
                        pltpu.VMEM((len(DA_GROUPS), tm, LANES), f32)],
        compiler_params=_params("parallel"), name="merge")(*args, *weights)


def _top_values(s, k):
    row = lax.broadcasted_iota(jnp.int32, (k, 1), 0)
    vals = jnp.zeros((k, s.shape[1]), f32)
    cur = s
    for r in range(k):
        m = jnp.max(cur, axis=0, keepdims=True)
        vals = jnp.where(row == r, m, vals)
        cur = jnp.where(cur == m, -jnp.inf, cur)
    return vals


def _route_body(q_ref, keys_ref, s2_ref, e2_ref, th_ref, e1_ref):
    kk = PEER_TOPK
    for h in range(PEER_HEADS):
        s1 = _dot_nt(keys_ref[2 * h], q_ref[:, (2 * h) * PEER_HALF:(2 * h + 1) * PEER_HALF])
        s2 = _dot_nt(keys_ref[2 * h + 1], q_ref[:, (2 * h + 1) * PEER_HALF:(2 * h + 2) * PEER_HALF])
        v1 = _top_values(s1, kk)
        v2 = _top_values(s2, kk)
        row8 = lax.broadcasted_iota(jnp.int32, (8, 1), 0)
        slabs = [v1[0:1] + v2, v1[1:2] + v2[0:8]]
        for a in range(2, 8):
            slabs.append(jnp.where(row8 < kk // (a + 1), v1[a:a + 1] + v2[0:8], -jnp.inf))
        slabs.append(v1[8:16] + v2[0:1])
        best = _top_values(jnp.concatenate(slabs, axis=0), kk)
        tau = best[kk - 1:kk]
        zden = jnp.sum(jnp.exp(best - best[0:1]), axis=0, keepdims=True)
        th = jnp.full(s1.shape, jnp.inf, f32)
        for a in range(kk):
            ok = (v1[a:a + 1] + v2) >= tau
            th_a = jnp.min(jnp.where(ok, v2, jnp.inf), axis=0, keepdims=True)
            th = jnp.where(s1 == v1[a:a + 1], th_a, th)
        s2_ref[h] = s2
        e2_ref[h] = jnp.exp(s2 - v2[0:1])
        th_ref[h] = th
        e1_ref[h] = jnp.exp(s1 - v1[0:1]) / zden


def _route(q, keys, tm):
    t = q.shape[0]
    out = jax.ShapeDtypeStruct((PEER_HEADS, PEER_NKEYS, t), f32)
    spec = pl.BlockSpec((PEER_HEADS, PEER_NKEYS, tm), lambda i: (0, 0, i))
    return pl.pallas_call(
        _route_body, out_shape=[out] * 4, grid=(t // tm,),
        in_specs=[pl.BlockSpec((tm, q.shape[1]), lambda i: (i, 0)), _resident(keys.shape)],
        out_specs=[spec] * 4, compiler_params=_params("parallel"), name="peer_route")(q, keys)


def _gelu_tanh(x):
    return 0.5 * x * (1.0 + jnp.tanh(math.sqrt(2.0 / math.pi) * (x + 0.044715 * (x * x * x))))


def _experts_body(x_ref, u_ref, vt_ref, s2_ref, e2_ref, th_ref, e1_ref, h_ref, nf_ref, o_ref, acc, w_scr, *, lane_chunk):
    j = pl.program_id(1)
    eb = u_ref.shape[0]
    tm = x_ref.shape[0]

    @pl.when(j == 0)
    def _():
        acc[...] = jnp.zeros_like(acc)

    act_pre = _dot_nt(u_ref[...], x_ref[...])
    for sub in range(eb // PEER_NKEYS):
        i1 = j * (eb // PEER_NKEYS) + sub
        for tc in range(tm // lane_chunk):
            ts = slice(tc * lane_chunk, (tc + 1) * lane_chunk)
            gate = jnp.zeros((PEER_NKEYS, lane_chunk), f32)
            for h in range(PEER_HEADS):
                th = th_ref[h, pl.ds(i1, 1), ts]
                e1 = e1_ref[h, pl.ds(i1, 1), ts]
                gate = gate + jnp.where(s2_ref[h, :, ts] >= th, e2_ref[h, :, ts], 0.0) * e1
            a = act_pre[sub * PEER_NKEYS:(sub + 1) * PEER_NKEYS, ts]
            w_scr[sub * PEER_NKEYS:(sub + 1) * PEER_NKEYS, ts] = (gate * _gelu_tanh(a)).astype(bf16)
    acc[...] += _dot(vt_ref[...], w_scr[...])

    @pl.when(j == pl.num_programs(1) - 1)
    def _():
        y = h_ref[...] + acc[...].T
        o_ref[...] = y * lax.rsqrt(jnp.mean(y * y, axis=-1, keepdims=True) + EPS) * nf_ref[...]


def _experts(xn, u, vt, routes, hid, nf, tm, eb):
    t, d = xn.shape
    ne = u.shape[0]
    rspec = pl.BlockSpec((PEER_HEADS, PEER_NKEYS, tm), lambda i, j: (0, 0, i))
    return pl.pallas_call(
        functools.partial(_experts_body, lane_chunk=min(tm, 256)),
        out_shape=jax.ShapeDtypeStruct((t, d), f32),
        grid=(t // tm, ne // eb),
        in_specs=[pl.BlockSpec((tm, d), lambda i, j: (i, 0)),
                  pl.BlockSpec((eb, d), lambda i, j: (j, 0)),
                  pl.BlockSpec((d, eb), lambda i, j: (0, j)),
                  rspec, rspec, rspec, rspec,
                  pl.BlockSpec((tm, d), lambda i, j: (i, 0)),
                  _resident((1, d))],
        out_specs=pl.BlockSpec((tm, d), lambda i, j: (i, 0)),
        scratch_shapes=[pltpu.VMEM((d, tm), f32), pltpu.VMEM((eb, tm), bf16)],
        compiler_params=_params("parallel", "arbitrary"), name="peer_experts")(xn, u, vt, *routes, hid, nf)


def _tile(n, want):
    t = min(n, want)
    assert n % t == 0, (n, want)
    return t


def _layer(h, norm_mix, w_in, dn_conv, dn_a_log, dn_dt_bias, dn_out_norm, rel_bias, w_branch_dn, w_branch_da,
           w_out, norm_ffn, peer_wq, peer_sub_keys, peer_u, peer_v, norm_out):
    bsz, seq, d = h.shape
    t = bsz * seq
    x2 = h.reshape(t, d)
    tm = _tile(seq, 512)

    offs = np.cumsum([0, DN_W, DN_W, DN_W, DN_W, 2 * DN_HEADS, 2 * DN_HEADS, 3 * DA_GW, 3 * DA_GW, 3 * DA_GW, d, d])
    col = lambda a, b: w_in[:, a:b]
    pieces = [col(offs[0], offs[4]), jnp.pad(col(offs[4], offs[6]), ((0, 0), (0, LANES - 4 * DN_HEADS)))]
    for gi in range(len(DA_GROUPS)):
        for part in range(3):
            a = offs[6 + part] + gi * DA_GW
            pieces.append(col(a, a + DA_GW))
    pieces.append(col(offs[9], offs[11]))
    w_cat = jnp.concatenate(pieces, axis=1).astype(bf16)

    dn, ab, a0, a1, a2, gates = _inproj(x2, norm_mix.reshape(1, d), w_cat, bsz, seq, tm)

    pad_row = lambda v: jnp.pad(v.reshape(1, -1).astype(f32), ((0, 0), (0, LANES - v.size)))
    sb = _tile(seq, 512)
    qkv_hat, gb = _dnprep(dn.reshape(bsz, seq, -1), ab.reshape(bsz, seq, LANES), dn_conv.astype(f32),
                          pad_row(dn_a_log), pad_row(dn_dt_bias), sb)
    sbs = _tile(seq, 256)
    o_f = _delta_scan(qkv_hat, gb, 0, sbs).reshape(t, DN_W)
    o_b = _delta_scan(qkv_hat, gb, 1, sbs).reshape(t, DN_W)

    att = []
    for gi, (qkv_g, (window, dil)) in enumerate(zip((a0, a1, a2), DA_GROUPS)):
        tq = _tile(seq // dil, 128)
        att.append(_dilated_attention(qkv_g, _attn_bias(rel_bias, gi, window, dil, tq), window, dil, tq))

    hid, xn, q = _merge(x2, o_f, o_b, dn, gates, att, dn_out_norm.reshape(1, DN_D).astype(f32),
                        w_branch_dn.astype(bf16), w_branch_da.astype(bf16), w_out.astype(bf16),
                        norm_ffn.reshape(1, d), peer_wq.astype(bf16), bsz, seq, tm)

    keys = peer_sub_keys.reshape(2 * PEER_HEADS, PEER_NKEYS, PEER_HALF).astype(bf16)
    routes = _route(q, keys, _tile(t, 512))
    return _experts(xn, peer_u.astype(bf16), peer_v.astype(bf16).T, routes, hid, norm_out.reshape(1, d),
                    _tile(t, 512), 512).reshape(bsz, seq, d)


def kernel(x, norm_mix, w_in, dn_conv, dn_a_log, dn_dt_bias, dn_out_norm, rel_bias, w_branch_dn, w_branch_da, w_out,
           norm_ffn, peer_wq, peer_sub_keys, peer_u, peer_v, norm_final):
    depth = norm_mix.shape[0]
    assert depth == 1, "the final RMSNorm is fused into the last layer's expert kernel"
    return _layer(x, norm_mix[0], w_in[0], dn_conv[0], dn_a_log[0], dn_dt_bias[0], dn_out_norm[0], rel_bias,
                  w_branch_dn[0], w_branch_da[0], w_out[0], norm_ffn[0], peer_wq[0], peer_sub_keys[0],
                  peer_u[0], peer_v[0], norm_final)
```

```python
import functools
import math

import jax
import jax.numpy as jnp
import numpy as np
from jax import lax
from jax.experimental import pallas as pl
from jax.experimental.pallas import tpu as pltpu

f32 = jnp.float32
bf16 = jnp.bfloat16

EPS = 1e-6
NEG_INF = -1e30
DN_HEADS = 4
DN_D = 128
DN_W = DN_HEADS * DN_D
DN_CONV = 4
DN_CHUNK = 64
DA_GROUPS = ((128, 1), (512, 4), (2048, 16))
DA_HEADS = 4
DA_HD = 128
DA_GW = DA_HEADS * DA_HD
RP_BUCKETS = 32
RP_MAX_DIST = 1024
PEER_HEADS = 8
PEER_NKEYS = 128
PEER_HALF = 128
PEER_TOPK = 16
LANES = 128
VMEM_LIMIT = 56 * 1024 * 1024


def _dot(a, b):
    return jnp.dot(a, b, preferred_element_type=f32)


def _dot_nt(a, b):
    return lax.dot_general(a, b, (((1,), (1,)), ((), ())), preferred_element_type=f32)


def _bmm(a, b):
    return jnp.einsum('cij,cjk->cik', a, b, preferred_element_type=f32)


def _bmm_nt(a, b):
    return jnp.einsum('cik,cjk->cij', a, b, preferred_element_type=f32)


def _sigmoid(x):
    return 1.0 / (1.0 + jnp.exp(-x))


def _params(*sem, flags=None):
    return pltpu.CompilerParams(dimension_semantics=sem, vmem_limit_bytes=VMEM_LIMIT, flags=flags)


def _resident(shape):
    nd = len(shape)
    return pl.BlockSpec(shape, lambda *_: (0,) * nd, pipeline_mode=pl.Buffered(1))


def _inproj_body(x_ref, g_ref, w_ref, dn_ref, ab_ref, a0_ref, a1_ref, a2_ref, gt_ref, xn_scr, st_scr):
    tm = x_ref.shape[0]
    x = x_ref[...]
    ms = jnp.mean(x * x, axis=-1, keepdims=True)
    xn_scr[...] = (x * lax.rsqrt(ms + EPS) * g_ref[...]).astype(bf16)
    xb = xn_scr[...]
    for c in range(4):
        dn_ref[:, c * DN_W:(c + 1) * DN_W] = _dot(xb, w_ref[:, c * DN_W:(c + 1) * DN_W]).astype(bf16)
    base = 4 * DN_W
    ab_ref[...] = _dot(xb, w_ref[:, base:base + LANES])
    base += LANES
    for ref, (_, dil) in zip((a0_ref, a1_ref, a2_ref), DA_GROUPS):
        for c in range(3):
            acc = _dot(xb, w_ref[:, base + c * DA_GW:base + (c + 1) * DA_GW])
            if dil == 1:
                ref[0, 0, :, c * DA_GW:(c + 1) * DA_GW] = acc.astype(bf16)
            else:
                for hh in range(DA_HEADS):
                    st_scr[hh] = acc[:, hh * DA_HD:(hh + 1) * DA_HD]
                for r in range(dil):
                    for hh in range(DA_HEADS):
                        lo = c * DA_GW + hh * DA_HD
                        ref[0, r, :, lo:lo + DA_HD] = st_scr[hh, pl.ds(r, tm // dil, stride=dil), :].astype(bf16)
        base += 3 * DA_GW
    for c in range(4):
        gt_ref[:, c * DN_W:(c + 1) * DN_W] = _sigmoid(_dot(xb, w_ref[:, base + c * DN_W:base + (c + 1) * DN_W])).astype(bf16)


def _inproj(x2, g, w, bsz, seq, tm):
    t, d = x2.shape
    nw = w.shape[1]
    spb = seq // tm
    outs = [jax.ShapeDtypeStruct((t, 4 * DN_W), bf16), jax.ShapeDtypeStruct((t, LANES), f32)]
    specs = [pl.BlockSpec((tm, 4 * DN_W), lambda i: (i, 0)), pl.BlockSpec((tm, LANES), lambda i: (i, 0))]
    for _, dil in DA_GROUPS:
        outs.append(jax.ShapeDtypeStruct((bsz, dil, seq // dil, 3 * DA_GW), bf16))
        specs.append(pl.BlockSpec((1, dil, tm // dil, 3 * DA_GW), lambda i: (i // spb, 0, i % spb, 0)))
    outs.append(jax.ShapeDtypeStruct((t, 4 * DN_W), bf16))
    specs.append(pl.BlockSpec((tm, 4 * DN_W), lambda i: (i, 0)))
    return pl.pallas_call(
        _inproj_body, out_shape=outs, grid=(t // tm,),
        in_specs=[pl.BlockSpec((tm, d), lambda i: (i, 0)), _resident((1, d)), _resident((d, nw))],
        out_specs=specs,
        scratch_shapes=[pltpu.VMEM((tm, d), bf16), pltpu.VMEM((DA_HEADS, tm, DA_HD), f32)],
        compiler_params=_params("parallel"), name="inproj")(x2, g, w)


def _dnprep_body(main_ref, prev_ref, next_ref, ab_ref, cw_ref, al_ref, dtb_ref, qkv_ref, gb_ref, buf):
    i = pl.program_id(1)
    last = pl.num_programs(1) - 1
    sb = main_ref.shape[1]
    halo = prev_ref.shape[1]
    prev = prev_ref[0].astype(f32)[halo - 8:]
    nxt = next_ref[0].astype(f32)[:8]
    buf[0:8] = jnp.where(i > 0, prev, 0.0)
    buf[8:8 + sb] = main_ref[0].astype(f32)
    buf[8 + sb:16 + sb] = jnp.where(i < last, nxt, 0.0)
    cw = cw_ref[...]
    for hh in range(3 * DN_HEADS):
        cols = slice(hh * DN_D, (hh + 1) * DN_D)
        y = (cw[0:1, cols] * buf[pl.ds(6, sb), cols] + cw[1:2, cols] * buf[pl.ds(7, sb), cols]
             + cw[2:3, cols] * buf[pl.ds(8, sb), cols] + cw[3:4, cols] * buf[pl.ds(9, sb), cols])
        y = y * _sigmoid(y)
        if hh < 2 * DN_HEADS:
            scale = lax.rsqrt(jnp.sum(y * y, axis=-1, keepdims=True) + EPS)
            if hh < DN_HEADS:
                scale = scale * (DN_D ** -0.5)
            y = y * scale
        qkv_ref[0, :, cols] = y.astype(bf16)
    ab = ab_ref[0]
    z = ab + dtb_ref[...]
    softplus = jnp.maximum(z, 0.0) + jnp.log(1.0 + jnp.exp(-jnp.abs(z)))
    g = -jnp.exp(al_ref[...]) * softplus
    lane = lax.broadcasted_iota(jnp.int32, ab.shape, 1)
    gb_ref[0] = jnp.where(lane < 2 * DN_HEADS, g, _sigmoid(ab))


def _dnprep(dn3, ab3, cw, al, dtb, sb):
    bsz, seq, _ = dn3.shape
    nblk = seq // sb
    halo = 16
    per = sb // halo
    return pl.pallas_call(
        _dnprep_body,
        out_shape=[jax.ShapeDtypeStruct((bsz, seq, 3 * DN_W), bf16), jax.ShapeDtypeStruct((bsz, seq, LANES), f32)],
        grid=(bsz, nblk),
        in_specs=[pl.BlockSpec((1, sb, 3 * DN_W), lambda b, i: (b, i, 0)),
                  pl.BlockSpec((1, halo, 3 * DN_W), lambda b, i: (b, jnp.maximum(i * per - 1, 0), 0)),
                  pl.BlockSpec((1, halo, 3 * DN_W), lambda b, i: (b, jnp.minimum((i + 1) * per, seq // halo - 1), 0)),
                  pl.BlockSpec((1, sb, LANES), lambda b, i: (b, i, 0)),
                  _resident((DN_CONV, 3 * DN_W)), _resident((1, LANES)), _resident((1, LANES))],
        out_specs=[pl.BlockSpec((1, sb, 3 * DN_W), lambda b, i: (b, i, 0)),
                   pl.BlockSpec((1, sb, LANES), lambda b, i: (b, i, 0))],
        scratch_shapes=[pltpu.VMEM((sb + 16, 3 * DN_W), f32)],
        compiler_params=_params("parallel", "parallel"), name="dnprep")(dn3, dn3, dn3, ab3, cw, al, dtb)


def _unit_tri_inverse(lmat, rb, cb):
    n = lmat.shape[-1]
    eye = (rb == cb).astype(f32)
    nb = jnp.where(rb // 8 == cb // 8, lmat, 0.0).astype(bf16)
    n2 = _bmm(nb, nb)
    x = eye - nb.astype(f32)
    x = x + _bmm(x.astype(bf16), n2.astype(bf16))
    n4 = _bmm(n2.astype(bf16), n2.astype(bf16))
    x = x + _bmm(x.astype(bf16), n4.astype(bf16))
    b = 8
    while b < n:
        coup = jnp.where((rb // (2 * b) == cb // (2 * b)) & (rb // b != cb // b), lmat, 0.0).astype(bf16)
        xb = x.astype(bf16)
        x = x - _bmm(_bmm(xb, coup).astype(bf16), xb)
        b *= 2
    return x


def _delta_body(qkv_ref, gb_ref, o_ref, s_scr, u_scr, wq_scr, ik_scr, gl_scr, *, dirn):
    blk = pl.program_id(1)
    sb = qkv_ref.shape[1]
    ck = DN_CHUNK
    nc = sb // ck

    @pl.when(blk == 0)
    def _():
        s_scr[...] = jnp.zeros_like(s_scr)

    rb = lax.broadcasted_iota(jnp.int32, (ck, ck), 0)
    cb = lax.broadcasted_iota(jnp.int32, (ck, ck), 1)
    incl = (rb >= cb) if dirn == 0 else (rb <= cb)
    strict = (rb > cb) if dirn == 0 else (rb < cb)
    tri = jnp.broadcast_to(incl.astype(f32), (nc, ck, ck))
    gb3 = gb_ref[0].reshape(nc, ck, LANES)
    gcum = _bmm(tri, gb3)
    gcum_t = jnp.swapaxes(gcum, 1, 2)
    gtot = jnp.sum(gb3, axis=1, keepdims=True)

    for h in range(DN_HEADS):
        lg = dirn * DN_HEADS + h
        lbeta = 2 * DN_HEADS + lg
        gc = gcum[:, :, lg:lg + 1]
        gcr = gcum_t[:, lg:lg + 1, :]
        gt = gtot[:, :, lg:lg + 1]
        beta = gb3[:, :, lbeta:lbeta + 1]
        q = qkv_ref[0, :, h * DN_D:(h + 1) * DN_D].reshape(nc, ck, DN_D)
        k = qkv_ref[0, :, DN_W + h * DN_D:DN_W + (h + 1) * DN_D].reshape(nc, ck, DN_D)
        v = qkv_ref[0, :, 2 * DN_W + h * DN_D:2 * DN_W + (h + 1) * DN_D].reshape(nc, ck, DN_D)
        kf = k.astype(f32)
        kb = kf * beta
        egc = jnp.exp(gc)
        decay = jnp.where(incl, jnp.exp(jnp.where(incl, gc - gcr, 0.0)), 0.0)
        lmat = jnp.where(strict, _bmm_nt(kb.astype(bf16), k) * decay, 0.0)
        tinv = _unit_tri_inverse(lmat, rb, cb)
        rhs = jnp.concatenate([v.astype(f32) * beta, kb * egc], axis=-1).astype(bf16)
        sol = _bmm(tinv.astype(bf16), rhs)
        intra = _bmm_nt(q, k) * decay
        q_dec = q.astype(f32) * egc
        k_dec_t = jnp.swapaxes(kf * jnp.exp(gt - gc), 1, 2)
        u_scr[h] = sol[:, :, :DN_D]
        wq_scr[h, :, :ck, :] = sol[:, :, DN_D:].astype(bf16)
        wq_scr[h, :, ck:, :] = q_dec.astype(bf16)
        ik_scr[h, :, :ck, :] = intra.astype(bf16)
        ik_scr[h, :, ck:, :] = k_dec_t.astype(bf16)
        gl_scr[h] = jnp.broadcast_to(jnp.exp(gt), (nc, DN_D, LANES))

    order = range(nc) if dirn == 0 else range(nc - 1, -1, -1)
    for c in order:
        for h in range(DN_HEADS):
            s = s_scr[h]
            a1 = _dot(wq_scr[h, c], s.astype(bf16))
            v_new = u_scr[h, c] - a1[:ck]
            a2 = _dot(ik_scr[h, c], v_new.astype(bf16))
            o_ref[0, c * ck:(c + 1) * ck, h * DN_D:(h + 1) * DN_D] = a1[ck:] + a2[:ck]
            s_scr[h] = s * gl_scr[h, c] + a2[ck:]


def _delta_scan(qkv3, gb3, dirn, sb):
    bsz, seq, _ = qkv3.shape
    nblk = seq // sb
    nc = sb // DN_CHUNK
    if dirn == 0:
        imap = lambda b, i: (b, i, 0)
    else:
        imap = lambda b, i: (b, nblk - 1 - i, 0)
    return pl.pallas_call(
        functools.partial(_delta_body, dirn=dirn),
        out_shape=jax.ShapeDtypeStruct((bsz, seq, DN_W), f32),
        grid=(bsz, nblk),
        in_specs=[pl.BlockSpec((1, sb, 3 * DN_W), imap), pl.BlockSpec((1, sb, LANES), imap)],
        out_specs=pl.BlockSpec((1, sb, DN_W), imap),
        scratch_shapes=[pltpu.VMEM((DN_HEADS, DN_D, DN_D), f32),
                        pltpu.VMEM((DN_HEADS, nc, DN_CHUNK, DN_D), f32),
                        pltpu.VMEM((DN_HEADS, nc, 2 * DN_CHUNK, DN_D), bf16),
                        pltpu.VMEM((DN_HEADS, nc, DN_CHUNK + DN_D, DN_CHUNK), bf16),
                        pltpu.VMEM((DN_HEADS, nc, DN_D, LANES), f32)],
        compiler_params=_params("parallel", "arbitrary"), name=f"delta_scan_d{dirn}")(qkv3, gb3)


def _t5_bucket(rel):
    half = RP_BUCKETS // 2
    max_exact = half // 2
    n = np.abs(rel)
    large = max_exact + (np.log(np.maximum(n, 1) / max_exact) / math.log(RP_MAX_DIST / max_exact)
                         * (half - max_exact)).astype(np.int64)
    large = np.minimum(large, half - 1)
    return np.where(rel > 0, half, 0) + np.where(n < max_exact, n, large)


def _attn_body(q_ref, kp_ref, km_ref, kn_ref, vp_ref, vm_ref, vn_ref, bias_ref, o_ref, lse_ref, *, side):
    i = pl.program_id(2)
    last = pl.num_programs(2) - 1
    tq = q_ref.shape[2]
    tk = tq + 2 * side
    col = lax.broadcasted_iota(jnp.int32, (tq, tk), 1)
    lo = jnp.where(i == 0, side, 0)
    hi = jnp.where(i == last, tq + side, tk)
    in_seq = (col >= lo) & (col < hi)
    lse_ref[0, 0] = jnp.zeros((tq, LANES), f32)
    for h in range(DA_HEADS):
        sl = slice(h * DA_HD, (h + 1) * DA_HD)
        q = q_ref[0, 0, :, sl]
        kcat = jnp.concatenate([kp_ref[0, 0, :, sl], km_ref[0, 0, :, sl], kn_ref[0, 0, :, sl]], axis=0)
        vcat = jnp.concatenate([vp_ref[0, 0, :, sl], vm_ref[0, 0, :, sl], vn_ref[0, 0, :, sl]], axis=0)
        s = _dot_nt(q, kcat) * (DA_HD ** -0.5) + bias_ref[h]
        s = jnp.where(in_seq, s, NEG_INF)
        m = jnp.max(s, axis=-1, keepdims=True)
        p = jnp.exp(s - m)
        l = jnp.sum(p, axis=-1, keepdims=True)
        o_ref[0, 0, :, sl] = (_dot(p.astype(bf16), vcat) / l).astype(bf16)
        lse_ref[0, 0, :, h:h + 1] = m + jnp.log(l)


def _dilated_attention(qkv, bias, window, dil, tq):
    bsz, _, length, _ = qkv.shape
    side = window // (2 * dil)
    per = tq // side
    nq = length // tq
    nside = length // side
    main = lambda c: pl.BlockSpec((1, 1, tq, DA_GW), lambda b, r, i: (b, r, i, c))
    prev = lambda c: pl.BlockSpec((1, 1, side, DA_GW), lambda b, r, i: (b, r, jnp.maximum(i * per - 1, 0), c))
    nxt = lambda c: pl.BlockSpec((1, 1, side, DA_GW), lambda b, r, i: (b, r, jnp.minimum((i + 1) * per, nside - 1), c))
    return pl.pallas_call(
        functools.partial(_attn_body, side=side),
        out_shape=[jax.ShapeDtypeStruct((bsz, dil, length, DA_GW), bf16),
                   jax.ShapeDtypeStruct((bsz, dil, length, LANES), f32)],
        grid=(bsz, dil, nq),
        in_specs=[main(0), prev(1), main(1), nxt(1), prev(2), main(2), nxt(2), _resident(bias.shape)],
        out_specs=[pl.BlockSpec((1, 1, tq, DA_GW), lambda b, r, i: (b, r, i, 0)),
                   pl.BlockSpec((1, 1, tq, LANES), lambda b, r, i: (b, r, i, 0))],
        compiler_params=_params("parallel", "parallel", "parallel"), name=f"dil_attn_d{dil}",
    )(qkv, qkv, qkv, qkv, qkv, qkv, qkv, bias)


def _attn_bias(rel_bias, gi, window, dil, tq):
    side = window // (2 * dil)
    rel = np.arange(tq + 2 * side)[None, :] - side - np.arange(tq)[:, None]
    table = rel_bias[:, gi * DA_HEADS:(gi + 1) * DA_HEADS].T.astype(f32)
    offs = np.arange(-side, side + 1)
    onehot = (_t5_bucket(offs * dil)[None, :] == np.arange(RP_BUCKETS)[:, None]).astype(np.float32)
    per_off = jnp.dot(table, jnp.asarray(onehot), precision=lax.Precision.HIGHEST)
    per_off = jnp.concatenate([per_off, jnp.full((DA_HEADS, 1), NEG_INF, f32)], axis=1)
    relj = lax.broadcasted_iota(jnp.int32, rel.shape, 1) - side - lax.broadcasted_iota(jnp.int32, rel.shape, 0)
    idx = jnp.where(jnp.abs(relj) <= side, relj + side, 2 * side + 1)
    band = (idx[None] == lax.broadcasted_iota(jnp.int32, (2 * side + 2, 1, 1), 0)).astype(f32)
    return jnp.einsum('ho,oqk->hqk', per_off, band, precision=lax.Precision.HIGHEST)


def _merge_body(x_ref, of_ref, ob_ref, z_ref, gt_ref, o0_ref, l0_ref, o1_ref, l1_ref, o2_ref, l2_ref,
                onorm_ref, wdn_ref, wda_ref, wout_ref, nffn_ref, wq_ref,
                h_ref, xnt_ref, q_ref, o_scr, l_scr):
    tm = x_ref.shape[0]
    parts = []
    for h in range(DN_HEADS):
        sl = slice(h * DN_D, (h + 1) * DN_D)
        o = of_ref[:, sl] + ob_ref[:, sl]
        o = o * lax.rsqrt(jnp.mean(o * o, axis=-1, keepdims=True) + EPS) * onorm_ref[...]
        z = z_ref[:, sl].astype(f32)
        parts.append((o * (z * _sigmoid(z))).astype(bf16))
    o_dn = jnp.concatenate(parts, axis=-1)
    for gi, (o_ref, l_ref, (_, dil)) in enumerate(zip((o0_ref, o1_ref, o2_ref), (l0_ref, l1_ref, l2_ref), DA_GROUPS)):
        for r in range(dil):
            rows = pl.ds(r, tm // dil, stride=dil) if dil > 1 else slice(None)
            for h in range(DA_HEADS):
                o_scr[gi, h, rows, :] = o_ref[0, r, :, h * DA_HD:(h + 1) * DA_HD].astype(f32)
            l_scr[gi, rows, :] = l_ref[0, r]
    lses = [l_scr[gi] for gi in range(len(DA_GROUPS))]
    mx = jnp.maximum(jnp.maximum(lses[0], lses[1]), lses[2])
    ws = [jnp.exp(l - mx) for l in lses]
    inv = 1.0 / (ws[0] + ws[1] + ws[2])
    parts = []
    for h in range(DA_HEADS):
        acc = sum(ws[gi][:, h:h + 1] * o_scr[gi, h] for gi in range(len(DA_GROUPS)))
        parts.append((acc * inv[:, h:h + 1]).astype(bf16))
    o_da = jnp.concatenate(parts, axis=-1)
    d = x_ref.shape[1]
    merged = (gt_ref[:, :d].astype(f32) * _dot(o_dn, wdn_ref[...])
              + gt_ref[:, d:].astype(f32) * _dot(o_da, wda_ref[...]))
    hid = x_ref[...] + _dot(merged.astype(bf16), wout_ref[...])
    h_ref[...] = hid
    xn = hid * lax.rsqrt(jnp.mean(hid * hid, axis=-1, keepdims=True) + EPS) * nffn_ref[...]
    xnt_ref[...] = xn.T.astype(bf16)
    q_ref[...] = _dot(xn.astype(bf16), wq_ref[...]).astype(bf16)


def _merge(x2, o_f, o_b, dn, gates, att, onorm, wdn, wda, wout, nffn, wq, bsz, seq, tm):
    t, d = x2.shape
    spb = seq // tm
    row = lambda w: pl.BlockSpec((tm, w), lambda i: (i, 0))
    in_specs = [row(d), row(DN_W), row(DN_W), pl.BlockSpec((tm, DN_W), lambda i: (i, 3)), row(2 * d)]
    args = [x2, o_f, o_b, dn, gates]
    for (o_g, lse_g), (_, dil) in zip(att, DA_GROUPS):
        in_specs.append(pl.BlockSpec((1, dil, tm // dil, DA_GW), lambda i: (i // spb, 0, i % spb, 0)))
        in_specs.append(pl.BlockSpec((1, dil, tm // dil, LANES), lambda i: (i // spb, 0, i % spb, 0)))
        args += [o_g, lse_g]
    weights = [onorm, wdn, wda, wout, nffn, wq]
    in_specs += [_resident(w.shape) for w in weights]
    nq = wq.shape[1]
    return pl.pallas_call(
        _merge_body,
        out_shape=[jax.ShapeDtypeStruct((t, d), f32), jax.ShapeDtypeStruct((d, t), bf16),
                   jax.ShapeDtypeStruct((t, nq), bf16)],
        grid=(t // tm,), in_specs=in_specs,
        out_specs=[row(d), pl.BlockSpec((d, tm), lambda i: (0, i)), row(nq)],
        scratch_shapes=[pltpu.VMEM((len(DA_GROUPS), DA_HEADS, tm, DA_HD), f32),
                        pltpu.VMEM((len(DA_GROUPS), tm, LANES), f32)],
        compiler_params=_params("parallel"), name="merge")(*args, *weights)


def _top_values(s, k, with_rank=False):
    row = lax.broadcasted_iota(jnp.int32, (k, 1), 0)
    vals = jnp.zeros((k, s.shape[1]), f32)
    rank = jnp.full(s.shape, float(k), f32)
    cur = s
    for r in range(k):
        m = jnp.max(cur, axis=0, keepdims=True)
        vals = jnp.where(row == r, m, vals)
        hit = cur == m
        if with_rank:
            rank = jnp.where(hit, float(r), rank)
        cur = jnp.where(hit, -jnp.inf, cur)
    return (vals, rank) if with_rank else vals


def _route_body(q_ref, keys_ref, r2_ref, e2_ref, n1_ref, e1_ref):
    kk = PEER_TOPK
    for h in range(PEER_HEADS):
        s1 = _dot_nt(keys_ref[2 * h], q_ref[:, (2 * h) * PEER_HALF:(2 * h + 1) * PEER_HALF])
        s2 = _dot_nt(keys_ref[2 * h + 1], q_ref[:, (2 * h + 1) * PEER_HALF:(2 * h + 2) * PEER_HALF])
        v1 = _top_values(s1, kk)
        v2, rank2 = _top_values(s2, kk, with_rank=True)
        row8 = lax.broadcasted_iota(jnp.int32, (8, 1), 0)
        slabs = [v1[0:1] + v2, v1[1:2] + v2[0:8]]
        for a in range(2, 8):
            slabs.append(jnp.where(row8 < kk // (a + 1), v1[a:a + 1] + v2[0:8], -jnp.inf))
        slabs.append(v1[8:16] + v2[0:1])
        best = _top_values(jnp.concatenate(slabs, axis=0), kk)
        tau = best[kk - 1:kk]
        zden = jnp.sum(jnp.exp(best - best[0:1]), axis=0, keepdims=True)
        n1 = jnp.zeros(s1.shape, f32)
        for a in range(kk):
            n_a = jnp.sum(jnp.where((v1[a:a + 1] + v2) >= tau, 1.0, 0.0), axis=0, keepdims=True)
            n1 = jnp.where(s1 == v1[a:a + 1], n_a, n1)
        r2_ref[h] = rank2.astype(bf16)
        e2_ref[h] = jnp.exp(s2 - v2[0:1]).astype(bf16)
        n1_ref[h] = n1
        e1_ref[h] = jnp.exp(s1 - v1[0:1]) / zden


def _route(q, keys, tm):
    t = q.shape[0]
    shape = (PEER_HEADS, PEER_NKEYS, t)
    spec = pl.BlockSpec((PEER_HEADS, PEER_NKEYS, tm), lambda i: (0, 0, i))
    return pl.pallas_call(
        _route_body,
        out_shape=[jax.ShapeDtypeStruct(shape, bf16), jax.ShapeDtypeStruct(shape, bf16),
                   jax.ShapeDtypeStruct(shape, f32), jax.ShapeDtypeStruct(shape, f32)],
        grid=(t // tm,),
        in_specs=[pl.BlockSpec((tm, q.shape[1]), lambda i: (i, 0)), _resident(keys.shape)],
        out_specs=[spec] * 4, compiler_params=_params("parallel"), name="peer_route")(q, keys)


def _gelu_tanh(x):
    return 0.5 * x * (1.0 + jnp.tanh(math.sqrt(2.0 / math.pi) * (x + 0.044715 * (x * x * x))))


def _experts_body(xt_ref, u_ref, vt_ref, r2_ref, e2_ref, n1_ref, e1_ref, h_ref, nf_ref, o_ref,
                  acc, act0, act1, w0, w1, *, lane_chunk):
    p = pl.program_id(1)
    last = pl.num_programs(1) - 1
    eb = act0.shape[0]
    tm = xt_ref.shape[1]
    nsub = eb // PEER_NKEYS

    @pl.when(p == 0)
    def _():
        acc[...] = jnp.zeros_like(acc)
        w0[...] = jnp.zeros_like(w0)
        w1[...] = jnp.zeros_like(w1)
        act1[...] = jnp.zeros_like(act1)

    def stage_a(half, act):
        act[...] = _dot(u_ref[half * eb:(half + 1) * eb, :], xt_ref[...])

    def stage_b(j, act, w):
        for sub in range(nsub):
            i1 = jnp.clip(j * nsub + sub, 0, PEER_NKEYS - 1)
            for tc in range(tm // lane_chunk):
                ts = slice(tc * lane_chunk, (tc + 1) * lane_chunk)
                gate = jnp.zeros((PEER_NKEYS, lane_chunk), bf16)
                for h in range(PEER_HEADS):
                    n1 = jnp.broadcast_to(n1_ref[h, pl.ds(i1, 1), ts], (PEER_NKEYS, lane_chunk)).astype(bf16)
                    e1 = jnp.broadcast_to(e1_ref[h, pl.ds(i1, 1), ts], (PEER_NKEYS, lane_chunk)).astype(bf16)
                    gate = gate + jnp.where(r2_ref[h, :, ts] < n1, e2_ref[h, :, ts], jnp.zeros((), bf16)) * e1
                rows = slice(sub * PEER_NKEYS, (sub + 1) * PEER_NKEYS)
                w[rows, ts] = gate * _gelu_tanh(act[rows, ts].astype(bf16))

    def stage_c(half, w):
        acc[...] += _dot(vt_ref[:, half * eb:(half + 1) * eb], w[...])

    stage_b(2 * p - 1, act1, w1)
    stage_c(0, w0)
    stage_a(0, act0)
    stage_b(2 * p, act0, w0)
    stage_c(1, w1)
    stage_a(1, act1)

    @pl.when(p == last)
    def _():
        y = h_ref[...] + acc[...].T
        o_ref[...] = y * lax.rsqrt(jnp.mean(y * y, axis=-1, keepdims=True) + EPS) * nf_ref[...]


_EXPERT_FLAGS = None


def _experts(xn_t, u, vt, routes, hid, nf, tm, eb):
    d, t = xn_t.shape
    npair = u.shape[0] // (2 * eb)
    rspec = pl.BlockSpec((PEER_HEADS, PEER_NKEYS, tm), lambda i, p: (0, 0, i))
    return pl.pallas_call(
        functools.partial(_experts_body, lane_chunk=min(tm, 256)),
        out_shape=jax.ShapeDtypeStruct((t, d), f32),
        grid=(t // tm, npair + 1),
        in_specs=[pl.BlockSpec((d, tm), lambda i, p: (0, i)),
                  pl.BlockSpec((2 * eb, d), lambda i, p: (jnp.minimum(p, npair - 1), 0)),
                  pl.BlockSpec((d, 2 * eb), lambda i, p: (0, jnp.maximum(p - 1, 0))),
                  rspec, rspec, rspec, rspec,
                  pl.BlockSpec((tm, d), lambda i, p: (i, 0)),
                  _resident((1, d))],
        out_specs=pl.BlockSpec((tm, d), lambda i, p: (i, 0)),
        scratch_shapes=[pltpu.VMEM((d, tm), f32), pltpu.VMEM((eb, tm), f32), pltpu.VMEM((eb, tm), f32),
                        pltpu.VMEM((eb, tm), bf16), pltpu.VMEM((eb, tm), bf16)],
        compiler_params=_params("parallel", "arbitrary", flags=_EXPERT_FLAGS), name="peer_experts",
    )(xn_t, u, vt, *routes, hid, nf)


def _tile(n, want):
    t = min(n, want)
    assert n % t == 0, (n, want)
    return t


def _layer(h, norm_mix, w_in, dn_conv, dn_a_log, dn_dt_bias, dn_out_norm, rel_bias, w_branch_dn, w_branch_da,
           w_out, norm_ffn, peer_wq, peer_sub_keys, peer_u, peer_v, norm_out):
    bsz, seq, d = h.shape
    t = bsz * seq
    x2 = h.reshape(t, d)
    tm = _tile(seq, 512)

    offs = np.cumsum([0, DN_W, DN_W, DN_W, DN_W, 2 * DN_HEADS, 2 * DN_HEADS, 3 * DA_GW, 3 * DA_GW, 3 * DA_GW, d, d])
    col = lambda a, b: w_in[:, a:b]
    pieces = [col(offs[0], offs[4]), jnp.pad(col(offs[4], offs[6]), ((0, 0), (0, LANES - 4 * DN_HEADS)))]
    for gi in range(len(DA_GROUPS)):
        for part in range(3):
            a = offs[6 + part] + gi * DA_GW
            pieces.append(col(a, a + DA_GW))
    pieces.append(col(offs[9], offs[11]))
    w_cat = jnp.concatenate(pieces, axis=1).astype(bf16)

    dn, ab, a0, a1, a2, gates = _inproj(x2, norm_mix.reshape(1, d), w_cat, bsz, seq, tm)

    pad_row = lambda v: jnp.pad(v.reshape(1, -1).astype(f32), ((0, 0), (0, LANES - v.size)))
    sb = _tile(seq, 512)
    qkv_hat, gb = _dnprep(dn.reshape(bsz, seq, -1), ab.reshape(bsz, seq, LANES), dn_conv.astype(f32),
                          pad_row(dn_a_log), pad_row(dn_dt_bias), sb)
    sbs = _tile(seq, 256)
    o_f = _delta_scan(qkv_hat, gb, 0, sbs).reshape(t, DN_W)
    o_b = _delta_scan(qkv_hat, gb, 1, sbs).reshape(t, DN_W)

    att = []
    for gi, (qkv_g, (window, dil)) in enumerate(zip((a0, a1, a2), DA_GROUPS)):
        tq = _tile(seq // dil, 128)
        att.append(_dilated_attention(qkv_g, _attn_bias(rel_bias, gi, window, dil, tq), window, dil, tq))

    hid, xn, q = _merge(x2, o_f, o_b, dn, gates, att, dn_out_norm.reshape(1, DN_D).astype(f32),
                        w_branch_dn.astype(bf16), w_branch_da.astype(bf16), w_out.astype(bf16),
                        norm_ffn.reshape(1, d), peer_wq.astype(bf16), bsz, seq, tm)

    keys = peer_sub_keys.reshape(2 * PEER_HEADS, PEER_NKEYS, PEER_HALF).astype(bf16)
    routes = _route(q, keys, _tile(t, 512))
    return _experts(xn, peer_u.astype(bf16), peer_v.astype(bf16).T, routes, hid, norm_out.reshape(1, d),
                    _tile(t, 512), 512).reshape(bsz, seq, d)


def kernel(x, norm_mix, w_in, dn_conv, dn_a_log, dn_dt_bias, dn_out_norm, rel_bias, w_branch_dn, w_branch_da, w_out,
           norm_ffn, peer_wq, peer_sub_keys, peer_u, peer_v, norm_final):
    depth = norm_mix.shape[0]
    assert depth == 1, "the final RMSNorm is fused into the last layer's expert kernel"
    return _layer(x, norm_mix[0], w_in[0], dn_conv[0], dn_a_log[0], dn_dt_bias[0], dn_out_norm[0], rel_bias,
                  w_branch_dn[0], w_branch_da[0], w_out[0], norm_ffn[0], peer_wq[0], peer_sub_keys[0],
                  peer_u[0], peer_v[0], norm_final)
```
